```python
import math
import jax
import jax.numpy as jnp
from jax import lax
import numpy as np

D_MODEL = 4096
BATCH = 4
SEQ = 2048
DEPTH = 2
DEC_BATCH = 32
DEC_SEQ = 4
PAST_LEN = 16384
PAGE_SIZE = 128

HEAD_DIM = 128
H_A = 20
KV_A = 4
H_B = 12
H_ATT = H_A + H_B
WINDOW_A = 128
DILATED_BRANCHES = ((128, 1), (512, 4), (2048, 16))
WINDOW_B_MAX = 2048
BAND = 128
N_BUCKETS = 32
RELPOS_MAX_DIST = 2048
ATT_COLS = (H_A * HEAD_DIM, KV_A * HEAD_DIM, KV_A * HEAD_DIM, H_B * HEAD_DIM, H_B * HEAD_DIM, H_B * HEAD_DIM)

RW_HEAD = 64
H_C = D_MODEL // RW_HEAD
LORA_DECAY = 128
LORA_AAA = 128
LORA_GATE = 480
GN_EPS = 64e-5

PEER_HEADS = 8
PEER_DKEY = 256
PEER_NKEYS = 128
PEER_EXPERTS = PEER_NKEYS * PEER_NKEYS
PEER_TOPK = 16
PEER_BLOCK = 128

LN_EPS = 1e-5
DEEPNORM_ALPHA = (2.0 * DEPTH) ** 0.25
DEEPNORM_BETA = (8.0 * DEPTH) ** -0.25

kernel_name = 'hybrid_swa_dilated_rwkv7_peer_step'


def layer_norm(x, g, b):
    xf = x.astype(jnp.float32)
    mu = jnp.mean(xf, axis=-1, keepdims=True)
    var = jnp.mean(jnp.square(xf - mu), axis=-1, keepdims=True)
    return ((xf - mu) * lax.rsqrt(var + LN_EPS) * g + b).astype(x.dtype)


def relpos_bucket(dist):
    n = jnp.maximum(jnp.asarray(dist, jnp.int32), 0)
    exact = N_BUCKETS // 2
    nf = jnp.maximum(n, 1).astype(jnp.float32)
    large = exact + (jnp.log(nf / exact) / math.log(RELPOS_MAX_DIST / exact) * (N_BUCKETS - exact)).astype(jnp.int32)
    return jnp.where(n < exact, n, jnp.minimum(large, N_BUCKETS - 1))


def relpos_bias(tab, dist):
    return jnp.moveaxis(tab[relpos_bucket(dist)].astype(jnp.float32), -1, 0)


def softmax_lse(s, sink=None):
    if sink is not None:
        s = jnp.concatenate([s, jnp.broadcast_to(sink.astype(jnp.float32), s.shape[:-1] + (1,))], axis=-1)
    lse = jax.nn.logsumexp(s, axis=-1)
    p = jnp.exp(s - lse[..., None])
    if sink is not None:
        p = p[..., :-1]
    return p, lse


def band_rows(x):
    b, l = x.shape[:2]
    nb = l // BAND
    pad = [(0, 0), (BAND, 0)] + [(0, 0)] * (x.ndim - 2)
    xb = jnp.pad(x, pad).reshape((b, nb + 1, BAND) + x.shape[2:])
    return jnp.concatenate([xb[:, :-1], xb[:, 1:]], axis=2)


def banded_attention(q, k, v, tab, dil, sink):
    b, l, h, hd = q.shape
    kvh = k.shape[2]
    rep = h // kvh
    lp = -(-l // BAND) * BAND
    nb = lp // BAND

    def pad_tail(z):
        return jnp.pad(z, [(0, 0), (0, lp - l)] + [(0, 0)] * (z.ndim - 2))

    qb = pad_tail(q).reshape(b, nb, BAND, kvh, rep, hd)
    kb = band_rows(pad_tail(k))
    vb = band_rows(pad_tail(v))
    sub = np.arange(BAND)[:, None] + BAND - np.arange(2 * BAND)[None, :]
    in_win = (sub >= 0) & (sub <= BAND)
    mask = in_win[None] & ((np.arange(nb)[:, None, None] > 0) | (np.arange(2 * BAND) >= BAND)[None, None, :])
    s = jnp.einsum('bnqgrd,bnkgd->bngrqk', qb, kb, preferred_element_type=jnp.float32) * hd ** -0.5
    s = s + relpos_bias(tab, dil * sub).reshape(kvh, rep, BAND, 2 * BAND)
    s = jnp.where(mask[None, :, None, None], s, -jnp.inf)
    p, lse = softmax_lse(s, None if sink is None else sink.reshape(kvh, rep, 1, 1))
    o = jnp.einsum('bngrqk,bnkgd->bnqgrd', p.astype(v.dtype), vb).reshape(b, lp, h, hd)[:, :l]
    lse = lse.transpose(0, 1, 4, 2, 3).reshape(b, lp, h)[:, :l]
    return o, lse


def merge_branches(outs, lses):
    wts = jax.nn.softmax(jnp.stack(lses), axis=0)
    o = jnp.sum(wts[..., None] * jnp.stack(outs).astype(jnp.float32), axis=0)
    return o.astype(outs[0].dtype)


def dilated_prompt(q, k, v, tab):
    b, l = q.shape[:2]
    outs, lses = [], []
    for (_, d) in DILATED_BRANCHES:
        def to_sub(a):
            return a.reshape((b, l // d, d) + a.shape[2:]).swapaxes(1, 2).reshape((b * d, l // d) + a.shape[2:])

        def from_sub(a):
            return a.reshape((b, d, l // d) + a.shape[2:]).swapaxes(1, 2).reshape((b, l) + a.shape[2:])

        o, lse = banded_attention(to_sub(q), to_sub(k), to_sub(v), tab, d, None)
        outs.append(from_sub(o))
        lses.append(from_sub(lse))
    return merge_branches(outs, lses)


def window_attention_sample(q, k_new, v_new, k_buf, v_buf, tab, sink):
    b, s_len, h, hd = q.shape
    l = k_buf.shape[1]
    kvh = k_new.shape[2]
    rep = h // kvh
    kall = jnp.concatenate([k_buf.astype(k_new.dtype), k_new], axis=1)
    vall = jnp.concatenate([v_buf.astype(v_new.dtype), v_new], axis=1)
    dist = (l + np.arange(s_len))[:, None] - np.arange(l + s_len)[None, :]
    mask = (dist >= 0) & (dist <= WINDOW_A)
    qg = q.reshape(b, s_len, kvh, rep, hd)
    s = jnp.einsum('bqgrd,bkgd->bgrqk', qg, kall, preferred_element_type=jnp.float32) * hd ** -0.5
    s = s + relpos_bias(tab, dist).reshape(kvh, rep, s_len, l + s_len)
    s = jnp.where(mask, s, -jnp.inf)
    p, _ = softmax_lse(s, sink.reshape(kvh, rep, 1, 1))
    o = jnp.einsum('bgrqk,bkgd->bqgrd', p.astype(vall.dtype), vall).reshape(b, s_len, h, hd)
    return o, kall[:, -l:], vall[:, -l:]


def dilated_sample(q, k_new, v_new, k_buf, v_buf, tab):
    b, s_len, h, hd = q.shape
    l = k_buf.shape[1]
    kall = jnp.concatenate([k_buf.astype(k_new.dtype), k_new], axis=1)
    vall = jnp.concatenate([v_buf.astype(v_new.dtype), v_new], axis=1)
    outs, lses = [], []
    for (w, d) in DILATED_BRANCHES:
        j = np.arange(w // d + 1)
        idx = (l + np.arange(s_len))[:, None] - d * j[None, :]
        valid = idx >= 0
        idxc = np.maximum(idx, 0)
        kg = kall[:, idxc]
        vg = vall[:, idxc]
        s = jnp.einsum('bqhd,bqjhd->bhqj', q, kg, preferred_element_type=jnp.float32) * hd ** -0.5
        s = s + relpos_bias(tab, d * j)[:, None, :]
        s = jnp.where(valid[None, None], s, -jnp.inf)
        p, lse = softmax_lse(s, None)
        outs.append(jnp.einsum('bhqj,bqjhd->bqhd', p.astype(vg.dtype), vg))
        lses.append(lse.transpose(0, 2, 1))
    return merge_branches(outs, lses), kall[:, -l:], vall[:, -l:]


def attn_projections(x, w_in_att):
    b, t, _ = x.shape
    p = jnp.einsum('btd,de->bte', x, w_in_att)
    parts = jnp.split(p, np.cumsum(ATT_COLS)[:-1].tolist(), axis=-1)
    return tuple(z.reshape(b, t, -1, HEAD_DIM) for z in parts)


def attn_output(oa, ob, w_out_att):
    b, t = oa.shape[:2]
    o = jnp.concatenate([oa, ob], axis=2).reshape(b, t, H_ATT * HEAD_DIM)
    return jnp.einsum('bte,ed->btd', o, w_out_att)


def wkv7_scan(s0, r, w, k, v, a, bvec):
    def step(s, inp):
        r_t, w_t, k_t, v_t, a_t, b_t = inp
        sa = jnp.einsum('bhvk,bhk->bhv', s, a_t)
        s = s * w_t[:, :, None, :] + sa[..., None] * b_t[:, :, None, :] + v_t[..., None] * k_t[:, :, None, :]
        return s, jnp.einsum('bhvk,bhk->bhv', s, r_t)
    seq = tuple(jnp.swapaxes(z, 0, 1) for z in (r, w, k, v, a, bvec))
    s_fin, y = lax.scan(step, s0, seq)
    return jnp.swapaxes(y, 0, 1), s_fin


def rwkv7_mixer(x, x_prev, s0, mu, w_rkv, w0, w1, w2, a0, a1, a2, g1, g2, k_k, k_a, r_k, gn_g, gn_b, w_o):
    b, t, d = x.shape
    f32 = jnp.float32
    xx = jnp.concatenate([x_prev[:, None, :].astype(x.dtype), x[:, :-1]], axis=1) - x
    xr, xw, xk, xv, xa, xg = (x + xx * mu[i] for i in range(6))
    r, k, v = jnp.einsum('cbtd,cde->cbte', jnp.stack([xr, xk, xv]), w_rkv)
    w_log = -jax.nn.softplus(-(w0 + jnp.tanh(xw @ w1) @ w2).astype(f32)) - 0.5
    decay = jnp.exp(-jnp.exp(w_log))
    a = jax.nn.sigmoid((a0 + (xa @ a1) @ a2).astype(f32))
    g = jax.nn.sigmoid(xg @ g1) @ g2

    def heads(z):
        return z.reshape(b, t, H_C, RW_HEAD)

    kf = k.astype(f32)
    kk = heads(kf * k_k)
    kk = kk / jnp.maximum(jnp.sqrt(jnp.sum(kk * kk, axis=-1, keepdims=True)), 1e-12)
    kf = heads(kf * (1.0 + (a - 1.0) * k_a))
    rf = heads(r.astype(f32))
    vf = heads(v.astype(f32))
    y, s_fin = wkv7_scan(s0.astype(f32), rf, heads(decay), kf, vf, -kk, kk * heads(a))
    mean = jnp.mean(y, axis=-1, keepdims=True)
    var = jnp.mean(jnp.square(y - mean), axis=-1, keepdims=True)
    y = (y - mean) * lax.rsqrt(var + GN_EPS) * gn_g.reshape(H_C, RW_HEAD) + gn_b.reshape(H_C, RW_HEAD)
    y = y + jnp.sum(rf * kf * r_k, axis=-1, keepdims=True) * vf
    y = y.reshape(b, t, d).astype(x.dtype)
    return (y * g) @ w_o, x[:, -1], s_fin.astype(x.dtype)


def peer_ffn(x, w_q, sub_keys, u_tab, v_tab):
    b, t, d = x.shape
    n = b * t
    npad = -(-n // PEER_BLOCK) * PEER_BLOCK
    xf = jnp.pad(x.reshape(n, d), ((0, npad - n), (0, 0)))

    def block(xb):
        q = (xb @ w_q).reshape(PEER_BLOCK, PEER_HEADS, 2, PEER_DKEY // 2)
        s = jnp.einsum('thcd,hcnd->thcn', q, sub_keys, preferred_element_type=jnp.float32)
        sv, si = lax.top_k(s, PEER_TOPK)
        cand = sv[:, :, 0, :, None] + sv[:, :, 1, None, :]
        cv, ci = lax.top_k(cand.reshape(PEER_BLOCK, PEER_HEADS, PEER_TOPK * PEER_TOPK), PEER_TOPK)
        i1 = jnp.take_along_axis(si[:, :, 0], ci // PEER_TOPK, axis=-1)
        i2 = jnp.take_along_axis(si[:, :, 1], ci % PEER_TOPK, axis=-1)
        idx = i1 * PEER_NKEYS + i2
        gate = jax.nn.softmax(cv, axis=-1)
        hid = jnp.einsum('thkd,td->thk', u_tab[idx], xb, preferred_element_type=jnp.float32)
        coef = (gate * jax.nn.gelu(hid, approximate=False)).astype(xb.dtype)
        return jnp.einsum('thk,thkd->td', coef, v_tab[idx])

    y = lax.map(block, xf.reshape(npad // PEER_BLOCK, PEER_BLOCK, d))
    return y.reshape(npad, d)[:n].reshape(b, t, d)


def setup_inputs(seed: int = 0) -> dict:
    key = jax.random.key(seed)
    ks = iter(jax.random.split(key, 48))

    def nrm(shape, scale):
        return jax.random.normal(next(ks), shape, jnp.float32) * scale

    def unif(shape, lo, hi):
        return jax.random.uniform(next(ks), shape, jnp.float32, lo, hi)

    d = D_MODEL
    sc = d ** -0.5
    la = min(WINDOW_A, PAST_LEN)
    lb = min(WINDOW_B_MAX, PAST_LEN)
    w_in_att = jnp.concatenate([
        nrm((d, ATT_COLS[0]), sc), nrm((d, ATT_COLS[1]), sc), nrm((d, ATT_COLS[2]), sc * DEEPNORM_BETA),
        nrm((d, ATT_COLS[3]), sc), nrm((d, ATT_COLS[4]), sc), nrm((d, ATT_COLS[5]), sc * DEEPNORM_BETA)], axis=1)
    w_rkv = nrm((3, d, d), sc) * jnp.array([1.0, 1.0, DEEPNORM_BETA], jnp.float32)[:, None, None]
    return {
        'x_prompt': nrm((BATCH, SEQ, d), 1.0),
        'x_sample': nrm((DEC_BATCH, DEC_SEQ, d), 1.0),
        'cache_a_k': nrm((DEC_BATCH, la, KV_A, HEAD_DIM), 1.0),
        'cache_a_v': nrm((DEC_BATCH, la, KV_A, HEAD_DIM), DEEPNORM_BETA),
        'cache_b_k': nrm((DEC_BATCH, lb, H_B, HEAD_DIM), 1.0),
        'cache_b_v': nrm((DEC_BATCH, lb, H_B, HEAD_DIM), DEEPNORM_BETA),
        'state_shift': nrm((DEC_BATCH, d), 1.0),
        'state_wkv': nrm((DEC_BATCH, H_C, RW_HEAD, RW_HEAD), 0.5),
        'w_in_att': w_in_att,
        'w_out_att': nrm((H_ATT * HEAD_DIM, d), (H_ATT * HEAD_DIM) ** -0.5 * DEEPNORM_BETA),
        'att_sinks': nrm((H_A,), 0.5),
        'rel_bias': nrm((N_BUCKETS, H_ATT), 0.5),
        'rw_mu': unif((6, d), 0.0, 1.0),
        'rw_w_rkv': w_rkv,
        'rw_w0': unif((d,), -6.0, -1.0),
        'rw_w1': nrm((d, LORA_DECAY), sc),
        'rw_w2': nrm((LORA_DECAY, d), 0.1 * LORA_DECAY ** -0.5),
        'rw_a0': nrm((d,), 0.1),
        'rw_a1': nrm((d, LORA_AAA), sc),
        'rw_a2': nrm((LORA_AAA, d), 0.1 * LORA_AAA ** -0.5),
        'rw_g1': nrm((d, LORA_GATE), sc),
        'rw_g2': nrm((LORA_GATE, d), LORA_GATE ** -0.5),
        'rw_k_k': 0.85 + nrm((d,), 0.05),
        'rw_k_a': 1.0 + nrm((d,), 0.05),
        'rw_r_k': nrm((H_C, RW_HEAD), 0.1),
        'rw_gn_g': 1.0 + nrm((d,), 0.05),
        'rw_gn_b': nrm((d,), 0.02),
        'rw_w_o': nrm((d, d), sc * DEEPNORM_BETA),
        'peer_w_q': nrm((DEPTH, d, PEER_HEADS * PEER_DKEY), sc),
        'peer_sub_keys': nrm((DEPTH, PEER_HEADS, 2, PEER_NKEYS, PEER_DKEY // 2), (PEER_DKEY // 2) ** -0.5),
        'peer_u': nrm((DEPTH, PEER_EXPERTS, d), sc),
        'peer_v': nrm((DEPTH, PEER_EXPERTS, d), DEEPNORM_BETA * PEER_HEADS ** -0.5),
        'ln_g': 1.0 + nrm((DEPTH, 2, d), 0.05),
        'ln_b': nrm((DEPTH, 2, d), 0.02),
    }


def reference(x_prompt, x_sample, cache_a_k, cache_a_v, cache_b_k, cache_b_v, state_shift, state_wkv,
              w_in_att, w_out_att, att_sinks, rel_bias,
              rw_mu, rw_w_rkv, rw_w0, rw_w1, rw_w2, rw_a0, rw_a1, rw_a2, rw_g1, rw_g2,
              rw_k_k, rw_k_a, rw_r_k, rw_gn_g, rw_gn_b, rw_w_o,
              peer_w_q, peer_sub_keys, peer_u, peer_v, ln_g, ln_b):
    xp, xs = x_prompt, x_sample
    tab_a, tab_b = rel_bias[:, :H_A], rel_bias[:, H_A:]
    for layer in range(DEPTH):
        if layer % 2 == 0:
            qa, ka, va, qb, kb, vb = attn_projections(xp, w_in_att)
            oa = banded_attention(qa, ka, va, tab_a, 1, att_sinks)[0]
            ob = dilated_prompt(qb, kb, vb, tab_b)
            mix_p = attn_output(oa, ob, w_out_att)
            la_p = min(WINDOW_A, xp.shape[1])
            lb_p = min(WINDOW_B_MAX, xp.shape[1])
            a_k_p, a_v_p = ka[:, -la_p:], va[:, -la_p:]
            b_k_p, b_v_p = kb[:, -lb_p:], vb[:, -lb_p:]
            qa, ka, va, qb, kb, vb = attn_projections(xs, w_in_att)
            oa, a_k_s, a_v_s = window_attention_sample(qa, ka, va, cache_a_k, cache_a_v, tab_a, att_sinks)
            ob, b_k_s, b_v_s = dilated_sample(qb, kb, vb, cache_b_k, cache_b_v, tab_b)
            mix_s = attn_output(oa, ob, w_out_att)
        else:
            rw = (rw_mu, rw_w_rkv, rw_w0, rw_w1, rw_w2, rw_a0, rw_a1, rw_a2, rw_g1, rw_g2,
                  rw_k_k, rw_k_a, rw_r_k, rw_gn_g, rw_gn_b, rw_w_o)
            zero_state = jnp.zeros((xp.shape[0], H_C, RW_HEAD, RW_HEAD), jnp.float32)
            mix_p, shift_p, wkv_p = rwkv7_mixer(xp, jnp.zeros_like(xp[:, 0]), zero_state, *rw)
            mix_s, shift_s, wkv_s = rwkv7_mixer(xs, state_shift, state_wkv, *rw)
        xp = layer_norm(DEEPNORM_ALPHA * xp + mix_p, ln_g[layer, 0], ln_b[layer, 0])
        xs = layer_norm(DEEPNORM_ALPHA * xs + mix_s, ln_g[layer, 0], ln_b[layer, 0])
        xp = layer_norm(DEEPNORM_ALPHA * xp + peer_ffn(xp, peer_w_q[layer], peer_sub_keys[layer], peer_u[layer], peer_v[layer]),
                        ln_g[layer, 1], ln_b[layer, 1])
        xs = layer_norm(DEEPNORM_ALPHA * xs + peer_ffn(xs, peer_w_q[layer], peer_sub_keys[layer], peer_u[layer], peer_v[layer]),
                        ln_g[layer, 1], ln_b[layer, 1])
    return (xp, xs, a_k_p, a_v_p, b_k_p, b_v_p, shift_p, wkv_p, a_k_s, a_v_s, b_k_s, b_v_s, shift_s, wkv_s)
```

```python
import functools
import math

import jax
import jax.numpy as jnp
import numpy as np
from jax import lax
from jax.experimental import pallas as pl
from jax.experimental.pallas import tpu as pltpu

f32 = jnp.float32
bf16 = jnp.bfloat16
NEG_INF = float("-inf")

HEAD_DIM = 128
H_A = 20
KV_A = 4
H_B = 12
H_ATT = H_A + H_B
REP_A = H_A // KV_A
BAND = 128
DILATIONS = (1, 4, 16)
N_BUCKETS = 32
RELPOS_MAX_DIST = 2048
RW_HEAD = 64
GN_EPS = 64e-5
PEER_HEADS = 8
PEER_NKEYS = 128
PEER_TOPK = 16
LN_EPS = 1e-5
DEPTH = 2
DEEPNORM_ALPHA = (2.0 * DEPTH) ** 0.25

QA_BLK = 0
KA_BLK = H_A
VA_BLK = H_A + KV_A
QB_BLK = H_A + 2 * KV_A
KB_BLK = QB_BLK + H_B
VB_BLK = KB_BLK + H_B

VMEM_LIMIT_BYTES = 56 * 1024 * 1024
LANES = 128

_NT = (((1,), (1,)), ((), ()))
_TN = (((0,), (0,)), ((), ()))


def _params(*sem):
    return pltpu.CompilerParams(dimension_semantics=sem, vmem_limit_bytes=VMEM_LIMIT_BYTES)


def _dot(a, b, dims=None):
    a = a.astype(bf16)
    b = b.astype(bf16)
    if dims is None:
        return jnp.dot(a, b, preferred_element_type=f32)
    return lax.dot_general(a, b, dims, preferred_element_type=f32)


def _mm_kernel(a_ref, b_ref, o_ref, *, act):
    acc = _dot(a_ref[...], b_ref[...])
    if act == "tanh":
        acc = jnp.tanh(acc)
    elif act == "sigmoid":
        acc = jax.nn.sigmoid(acc)
    o_ref[...] = acc.astype(o_ref.dtype)


def matmul(a, b, *, lead=None, act=None, out_dtype=f32, tn=512):
    m, k = a.shape
    n = b.shape[-1]
    tm = min(m, 1024)
    tn = min(n, tn)
    if lead is None:
        b_spec = pl.BlockSpec((k, tn), lambda i, j: (0, j))
    else:
        b_spec = pl.BlockSpec((None, k, tn), lambda i, j: (lead, 0, j))
    return pl.pallas_call(
        functools.partial(_mm_kernel, act=act),
        grid=(m // tm, n // tn),
        in_specs=[pl.BlockSpec((tm, k), lambda i, j: (i, 0)), b_spec],
        out_specs=pl.BlockSpec((tm, tn), lambda i, j: (i, j)),
        out_shape=jax.ShapeDtypeStruct((m, n), out_dtype),
        compiler_params=_params("parallel", "parallel"),
    )(a, b)


def _add_ln_kernel(x_ref, y_ref, g_ref, b_ref, o_ref, ob_ref, *, y_transposed):
    y = y_ref[...]
    if y_transposed:
        y = y.T
    z = DEEPNORM_ALPHA * x_ref[...] + y
    mu = jnp.mean(z, axis=-1, keepdims=True)
    zc = z - mu
    var = jnp.mean(zc * zc, axis=-1, keepdims=True)
    out = zc * lax.rsqrt(var + LN_EPS) * g_ref[...] + b_ref[...]
    o_ref[...] = out
    ob_ref[...] = out.astype(bf16)


def add_ln(x, y, g, b, *, y_transposed=False):
    m, d = x.shape
    tr = min(m, 256)
    y_spec = pl.BlockSpec((d, tr), lambda i: (0, i)) if y_transposed else pl.BlockSpec((tr, d), lambda i: (i, 0))
    row = pl.BlockSpec((tr, d), lambda i: (i, 0))
    vec = pl.BlockSpec((1, d), lambda i: (0, 0))
    return pl.pallas_call(
        functools.partial(_add_ln_kernel, y_transposed=y_transposed),
        grid=(m // tr,),
        in_specs=[row, y_spec, vec, vec],
        out_specs=(row, row),
        out_shape=(jax.ShapeDtypeStruct((m, d), f32), jax.ShapeDtypeStruct((m, d), bf16)),
        compiler_params=_params("parallel"),
    )(x, y, g.reshape(1, d), b.reshape(1, d))


def _relpos_bucket(dist):
    n = jnp.maximum(jnp.asarray(dist, jnp.int32), 0)
    exact = N_BUCKETS // 2
    nf = jnp.maximum(n, 1).astype(f32)
    large = exact + (jnp.log(nf / exact) / math.log(RELPOS_MAX_DIST / exact) * (N_BUCKETS - exact)).astype(jnp.int32)
    return jnp.where(n < exact, n, jnp.minimum(large, N_BUCKETS - 1))


def _bias_table(tab, dist, valid):
    bias = jnp.moveaxis(tab[_relpos_bucket(dist)].astype(f32), -1, 0)
    return jnp.where(jnp.asarray(valid), bias, NEG_INF)


def _prompt_bias(rel_bias):
    sub = np.arange(BAND)[:, None] + BAND - np.arange(2 * BAND)[None, :]
    in_win = (sub >= 0) & (sub <= BAND)
    return jnp.stack([_bias_table(rel_bias, d * sub, in_win) for d in DILATIONS])


SAMPLE_ROWS = 8
KALL_SHORT = 2 * BAND
KALL_LONG = 5 * BAND


def _sample_bias(rel_bias, s_len):
    tab_a, tab_b = rel_bias[:, :H_A], rel_bias[:, H_A:]
    s = np.arange(SAMPLE_ROWS)[:, None]
    real = s < s_len

    i = np.arange(KALL_SHORT)[None, :]
    dist = np.where(i < BAND, BAND + s - i, s - (i - BAND))
    valid = (dist >= 0) & (dist <= BAND) & (i < BAND + s_len)
    dist_w = np.where(real, dist, 0)
    valid_w = np.where(real, valid, True)
    bias_a = _bias_table(tab_a, dist_w, valid_w)
    bias_1 = _bias_table(tab_b, dist_w, valid_w)

    i = np.arange(KALL_LONG)[None, :]
    nbuf = 4 * BAND
    j4 = (nbuf + s - i)
    valid4 = np.where(i < nbuf, (j4 % 4 == 0) & (j4 >= 4) & (j4 <= 4 * BAND), (i - nbuf) == s)
    dist4 = np.where(i < nbuf, j4, 0)
    bias_4 = _bias_table(tab_b, np.where(real, dist4, 0), np.where(real, valid4, True))
    sp, g = i // BAND, i % BAND
    valid16 = np.where(i < nbuf, sp == s, (i - nbuf) == s)
    dist16 = np.where(i < nbuf, 16 * (BAND - g), 0)
    bias_16 = _bias_table(tab_b, np.where(real, dist16, 0), np.where(real, valid16, True))
    return bias_a, bias_1, bias_4, bias_16


def _attn_prompt_kernel(sink_ref, q_ref, k_ref, v_ref, bias_ref, o_ref, qd, kd, vd, od, ld, on, ln, *, seq):
    h = pl.program_id(1)
    sink = sink_ref[h]
    scale = HEAD_DIM ** -0.5
    col = lax.broadcasted_iota(jnp.int32, (BAND, 2 * BAND), 1)
    kd[pl.ds(0, BAND), :] = jnp.zeros((BAND, HEAD_DIM), f32)
    vd[pl.ds(0, BAND), :] = jnp.zeros((BAND, HEAD_DIM), f32)

    def branch(bi, d):
        rows = seq // d
        nb = rows // BAND
        for r in range(d):
            src = pl.ds(r, rows, stride=d) if d > 1 else pl.ds(0, rows)
            qd[pl.ds(r * rows, rows), :] = q_ref[src, :]
            kd[pl.ds(BAND + r * rows, rows), :] = k_ref[src, :]
            vd[pl.ds(BAND + r * rows, rows), :] = v_ref[src, :]
        bias = bias_ref[bi]

        def block(i, carry):
            off = pl.multiple_of(i * BAND, BAND)
            q = qd[pl.ds(off, BAND), :]
            k = kd[pl.ds(off, 2 * BAND), :]
            v = vd[pl.ds(off, 2 * BAND), :]
            s = _dot(q, k, _NT) * scale + bias
            lim = jnp.where(i % nb == 0, BAND, 0)
            s = jnp.where(col < lim, NEG_INF, s)
            m = jnp.maximum(jnp.max(s, axis=-1, keepdims=True), sink)
            p = jnp.exp(s - m)
            l = jnp.sum(p, axis=-1, keepdims=True) + jnp.exp(sink - m)
            od[pl.ds(off, BAND), :] = _dot(p, v) / l
            ld[pl.ds(off, BAND), :] = jnp.broadcast_to(m + jnp.log(l), (BAND, HEAD_DIM))
            return carry

        lax.fori_loop(0, seq // BAND, block, 0)
        for r in range(d):
            dst = pl.ds(r, rows, stride=d) if d > 1 else pl.ds(0, rows)
            on[bi, dst, :] = od[pl.ds(r * rows, rows), :]
            ln[bi, dst, :] = ld[pl.ds(r * rows, rows), :]

    branch(0, DILATIONS[0])

    @pl.when(h < H_A)
    def _():
        o_ref[...] = on[0].astype(o_ref.dtype)

    @pl.when(h >= H_A)
    def _():
        branch(1, DILATIONS[1])
        branch(2, DILATIONS[2])
        l0, l1, l2 = ln[0], ln[1], ln[2]
        mx = jnp.maximum(jnp.maximum(l0, l1), l2)
        w0, w1, w2 = jnp.exp(l0 - mx), jnp.exp(l1 - mx), jnp.exp(l2 - mx)
        o = (w0 * on[0] + w1 * on[1] + w2 * on[2]) / (w0 + w1 + w2)
        o_ref[...] = o.astype(o_ref.dtype)


def attn_prompt(qkv, bias, sinks, *, batch, seq):
    def q_blk(b, h):
        return (b, jnp.where(h < H_A, QA_BLK + h, QB_BLK + h - H_A))

    def k_blk(b, h):
        return (b, jnp.where(h < H_A, KA_BLK + h // REP_A, KB_BLK + h - H_A))

    def v_blk(b, h):
        return (b, jnp.where(h < H_A, VA_BLK + h // REP_A, VB_BLK + h - H_A))

    blk = (seq, HEAD_DIM)
    return pl.pallas_call(
        functools.partial(_attn_prompt_kernel, seq=seq),
        grid=(batch, H_ATT),
        in_specs=[
            pl.BlockSpec(memory_space=pltpu.SMEM),
            pl.BlockSpec(blk, q_blk), pl.BlockSpec(blk, k_blk), pl.BlockSpec(blk, v_blk),
            pl.BlockSpec((len(DILATIONS), None, BAND, 2 * BAND), lambda b, h: (0, h, 0, 0)),
        ],
        out_specs=pl.BlockSpec(blk, lambda b, h: (b, h)),
        out_shape=jax.ShapeDtypeStruct((batch * seq, H_ATT * HEAD_DIM), bf16),
        scratch_shapes=[
            pltpu.VMEM((seq, HEAD_DIM), f32),
            pltpu.VMEM((seq + BAND, HEAD_DIM), f32),
            pltpu.VMEM((seq + BAND, HEAD_DIM), f32),
            pltpu.VMEM((seq, HEAD_DIM), f32),
            pltpu.VMEM((seq, HEAD_DIM), f32),
            pltpu.VMEM((len(DILATIONS), seq, HEAD_DIM), f32),
            pltpu.VMEM((len(DILATIONS), seq, HEAD_DIM), f32),
        ],
        compiler_params=_params("parallel", "arbitrary"),
    )(sinks, qkv, qkv, qkv, bias)


def _attn_sample_kernel(sink_ref, new_ref, cak_ref, cav_ref, cbk_ref, cbv_ref, cbk16_ref, cbv16_ref,
                        ba_ref, b1_ref, b4_ref, b16_ref, o_ref, q8, kall, vall, *, s_len):
    scale = HEAD_DIM ** -0.5
    q8[...] = jnp.zeros_like(q8)

    def cols(blk):
        return slice(blk * HEAD_DIM, (blk + 1) * HEAD_DIM)

    def attend(nk, bias, sink):
        s = _dot(q8[...], kall[pl.ds(0, nk), :], _NT) * scale + bias
        m = jnp.maximum(jnp.max(s, axis=-1, keepdims=True), sink)
        p = jnp.exp(s - m)
        l = jnp.sum(p, axis=-1, keepdims=True) + jnp.exp(sink - m)
        return _dot(p, vall[pl.ds(0, nk), :]) / l, m + jnp.log(l)

    def put_new(at, kblk, vblk):
        pad = jnp.zeros((BAND, HEAD_DIM), f32)
        kall[pl.ds(at, BAND), :] = pad
        vall[pl.ds(at, BAND), :] = pad
        kall[pl.ds(at, s_len), :] = new_ref[:, cols(kblk)]
        vall[pl.ds(at, s_len), :] = new_ref[:, cols(vblk)]

    for g in range(KV_A):
        kall[pl.ds(0, BAND), :] = cak_ref[:, cols(g)]
        vall[pl.ds(0, BAND), :] = cav_ref[:, cols(g)]
        put_new(BAND, KA_BLK + g, VA_BLK + g)
        for r in range(REP_A):
            h = g * REP_A + r
            q8[pl.ds(0, s_len), :] = new_ref[:, cols(QA_BLK + h)]
            o, _ = attend(KALL_SHORT, ba_ref[h], sink_ref[h])
            o_ref[:, cols(h)] = o[0:s_len]

    nbuf = 4 * BAND
    for hb in range(H_B):
        q8[pl.ds(0, s_len), :] = new_ref[:, cols(QB_BLK + hb)]
        kall[pl.ds(0, BAND), :] = cbk_ref[pl.ds(nbuf - BAND, BAND), cols(hb)]
        vall[pl.ds(0, BAND), :] = cbv_ref[pl.ds(nbuf - BAND, BAND), cols(hb)]
        put_new(BAND, KB_BLK + hb, VB_BLK + hb)
        o1, l1 = attend(KALL_SHORT, b1_ref[hb], NEG_INF)
        kall[pl.ds(0, nbuf), :] = cbk_ref[:, cols(hb)]
        vall[pl.ds(0, nbuf), :] = cbv_ref[:, cols(hb)]
        put_new(nbuf, KB_BLK + hb, VB_BLK + hb)
        o4, l4 = attend(KALL_LONG, b4_ref[hb], NEG_INF)
        for sp in range(s_len):
            kall[pl.ds(sp * BAND, BAND), :] = cbk16_ref[:, cols(sp * H_B + hb)]
            vall[pl.ds(sp * BAND, BAND), :] = cbv16_ref[:, cols(sp * H_B + hb)]
        o16, l16 = attend(KALL_LONG, b16_ref[hb], NEG_INF)
        mx = jnp.maximum(jnp.maximum(l1, l4), l16)
        w1, w4, w16 = jnp.exp(l1 - mx), jnp.exp(l4 - mx), jnp.exp(l16 - mx)
        o = (w1 * o1 + w4 * o4 + w16 * o16) / (w1 + w4 + w16)
        o_ref[:, cols(H_A + hb)] = o[0:s_len]


def attn_sample(qkv, cache_a_k, cache_a_v, cache_b_k, cache_b_v, biases, sinks, *, batch, s_len):
    la = cache_a_k.shape[1]
    lb = cache_b_k.shape[1]
    assert la == BAND and lb == 16 * BAND and s_len == 4
    wide = qkv.shape[1]
    new3 = qkv.reshape(batch, s_len, wide)
    ca_k = cache_a_k.reshape(batch, la, KV_A * HEAD_DIM)
    ca_v = cache_a_v.reshape(batch, la, KV_A * HEAD_DIM)
    cb_k = cache_b_k.reshape(batch, lb, H_B * HEAD_DIM)
    cb_v = cache_b_v.reshape(batch, lb, H_B * HEAD_DIM)
    cb_k16 = cache_b_k.reshape(batch, lb // 16, 16 * H_B * HEAD_DIM)
    cb_v16 = cache_b_v.reshape(batch, lb // 16, 16 * H_B * HEAD_DIM)
    bias_a, bias_1, bias_4, bias_16 = biases
    nbuf = 4 * BAND
    tail = pl.BlockSpec((None, nbuf, H_B * HEAD_DIM), lambda b: (b, lb // nbuf - 1, 0))
    grouped = pl.BlockSpec((None, lb // 16, s_len * H_B * HEAD_DIM), lambda b: (b, 0, 0))
    a_spec = pl.BlockSpec((None, la, KV_A * HEAD_DIM), lambda b: (b, 0, 0))

    def full(x):
        return pl.BlockSpec(x.shape, lambda b: (0,) * x.ndim)

    out = pl.pallas_call(
        functools.partial(_attn_sample_kernel, s_len=s_len),
        grid=(batch,),
        in_specs=[
            pl.BlockSpec(memory_space=pltpu.SMEM),
            pl.BlockSpec((None, s_len, wide), lambda b: (b, 0, 0)),
            a_spec, a_spec, tail, tail, grouped, grouped,
            full(bias_a), full(bias_1), full(bias_4), full(bias_16),
        ],
        out_specs=pl.BlockSpec((None, s_len, H_ATT * HEAD_DIM), lambda b: (b, 0, 0)),
        out_shape=jax.ShapeDtypeStruct((batch, s_len, H_ATT * HEAD_DIM), f32),
        scratch_shapes=[
            pltpu.VMEM((SAMPLE_ROWS, HEAD_DIM), f32),
            pltpu.VMEM((KALL_LONG, HEAD_DIM), f32),
            pltpu.VMEM((KALL_LONG, HEAD_DIM), f32),
        ],
        compiler_params=_params("parallel"),
    )(sinks, new3, ca_k, ca_v, cb_k, cb_v, cb_k16, cb_v16, bias_a, bias_1, bias_4, bias_16)
    return out.reshape(batch * s_len, H_ATT * HEAD_DIM)


def _top16(work, idx):
    rank = jnp.full(work.shape, float(PEER_TOPK), f32)
    vals = []
    sentinel = float(work.shape[0])
    for k in range(PEER_TOPK):
        m = jnp.max(work, axis=0, keepdims=True)
        first = jnp.min(jnp.where(work == m, idx, sentinel), axis=0, keepdims=True)
        hit = idx == first
        rank = jnp.where(hit, float(k), rank)
        work = jnp.where(hit, NEG_INF, work)
        vals.append(m)
    return jnp.concatenate(vals, axis=0), rank


def _peer_select_kernel(q_ref, keys_ref, r2_ref, e2_ref, n_ref, ca_ref):
    nk = PEER_NKEYS
    kidx = lax.broadcasted_iota(jnp.int32, (nk, LANES), 0).astype(f32)
    cidx = lax.broadcasted_iota(jnp.int32, (PEER_TOPK * PEER_TOPK, LANES), 0).astype(f32)
    for h in range(PEER_HEADS):
        s1 = _dot(keys_ref[h, 0], q_ref[:, (2 * h) * nk:(2 * h + 1) * nk], _NT)
        s2 = _dot(keys_ref[h, 1], q_ref[:, (2 * h + 1) * nk:(2 * h + 2) * nk], _NT)
        sv1, rank1 = _top16(s1, kidx)
        sv2, rank2 = _top16(s2, kidx)
        cand = jnp.concatenate([sv1[ka:ka + 1] + sv2 for ka in range(PEER_TOPK)], axis=0)
        _, rank_c = _top16(cand, cidx)
        chosen = rank_c < float(PEER_TOPK)
        e1 = jnp.exp(sv1 - sv1[0:1])
        e2 = jnp.exp(sv2 - sv2[0:1])
        eprod = jnp.concatenate([e1[ka:ka + 1] * e2 for ka in range(PEER_TOPK)], axis=0)
        z = jnp.sum(jnp.where(chosen, eprod, 0.0), axis=0, keepdims=True)
        count = jnp.where(chosen, 1.0, 0.0)
        ca_rank = e1 / z
        n_key = jnp.zeros((nk, LANES), f32)
        ca_key = jnp.zeros((nk, LANES), f32)
        for ka in range(PEER_TOPK):
            n_ka = jnp.sum(count[ka * PEER_TOPK:(ka + 1) * PEER_TOPK], axis=0, keepdims=True)
            hit = rank1 == float(ka)
            n_key = jnp.where(hit, n_ka, n_key)
            ca_key = jnp.where(hit, ca_rank[ka:ka + 1], ca_key)
        r2_ref[h * nk:(h + 1) * nk, :] = rank2
        e2_ref[h * nk:(h + 1) * nk, :] = jnp.where(rank2 < float(PEER_TOPK), jnp.exp(s2 - sv2[0:1]), 0.0)
        n_ref[:, h, :] = n_key
        ca_ref[:, h, :] = ca_key


def peer_select(q, sub_keys, layer):
    t = q.shape[0]
    hk = PEER_HEADS * PEER_NKEYS
    flat = pl.BlockSpec((hk, LANES), lambda i: (0, i))
    by_key = pl.BlockSpec((PEER_NKEYS, PEER_HEADS, LANES), lambda i: (0, 0, i))
    return pl.pallas_call(
        _peer_select_kernel,
        grid=(t // LANES,),
        in_specs=[
            pl.BlockSpec((LANES, q.shape[1]), lambda i: (i, 0)),
            pl.BlockSpec((None,) + sub_keys.shape[1:], lambda i: (layer, 0, 0, 0, 0)),
        ],
        out_specs=(flat, flat, by_key, by_key),
        out_shape=(
            jax.ShapeDtypeStruct((hk, t), f32), jax.ShapeDtypeStruct((hk, t), f32),
            jax.ShapeDtypeStruct((PEER_NKEYS, PEER_HEADS, t), f32),
            jax.ShapeDtypeStruct((PEER_NKEYS, PEER_HEADS, t), f32),
        ),
        compiler_params=_params("parallel"),
    )(q, sub_keys)


def _gelu(x):
    return 0.5 * x * (1.0 + lax.erf(x * (2.0 ** -0.5)))


def _peer_dense_kernel(x_ref, u_ref, vt_ref, r2_ref, e2_ref, n_ref, ca_ref, o_ref, *, ta):
    j = pl.program_id(1)
    nk = PEER_NKEYS

    @pl.when(j == 0)
    def _():
        o_ref[...] = jnp.zeros_like(o_ref)

    hid = _dot(u_ref[...], x_ref[...], _NT)
    coefs = []
    for aa in range(ta):
        gate = jnp.zeros((nk, hid.shape[1]), f32)
        for h in range(PEER_HEADS):
            n_row = n_ref[aa, h:h + 1, :]
            ca_row = ca_ref[aa, h:h + 1, :]
            r2 = r2_ref[h * nk:(h + 1) * nk, :]
            e2 = e2_ref[h * nk:(h + 1) * nk, :]
            gate = gate + jnp.where(r2 < n_row, e2 * ca_row, 0.0)
        coefs.append((gate * _gelu(hid[aa * nk:(aa + 1) * nk])).astype(bf16))
    coef = jnp.concatenate(coefs, axis=0) if ta > 1 else coefs[0]
    o_ref[...] += jnp.dot(vt_ref[...], coef, preferred_element_type=f32)


def peer_dense(xb, u_bf, vt_bf, sel, *, ta=2):
    t, d = xb.shape
    rank2, e2, n_key, ca_key = sel
    tm = min(t, 1024)
    te = ta * PEER_NKEYS
    hk = PEER_HEADS * PEER_NKEYS
    once = pl.Buffered(1)
    flat = pl.BlockSpec((hk, tm), lambda i, j: (0, i), pipeline_mode=once)
    by_key = pl.BlockSpec((ta, PEER_HEADS, tm), lambda i, j: (j, 0, i))
    return pl.pallas_call(
        functools.partial(_peer_dense_kernel, ta=ta),
        grid=(t // tm, PEER_NKEYS // ta),
        in_specs=[
            pl.BlockSpec((tm, d), lambda i, j: (i, 0), pipeline_mode=once),
            pl.BlockSpec((te, d), lambda i, j: (j, 0)),
            pl.BlockSpec((d, te), lambda i, j: (0, j)),
            flat, flat, by_key, by_key,
        ],
        out_specs=pl.BlockSpec((d, tm), lambda i, j: (0, i), pipeline_mode=once),
        out_shape=jax.ShapeDtypeStruct((d, t), f32),
        compiler_params=_params("parallel", "arbitrary"),
    )(xb, u_bf, vt_bf, rank2, e2, n_key, ca_key)


def peer_ffn_t(xb, w_q, sub_keys, u_bf, vt_bf, layer):
    q = matmul(xb, w_q, lead=layer)
    return peer_dense(xb, u_bf, vt_bf, peer_select(q, sub_keys, layer))


def _mix_kernel(x_ref, xs_ref, mu_ref, *o_refs):
    x = x_ref[...]
    xx = xs_ref[...] - x
    for i, o_ref in enumerate(o_refs):
        o_ref[...] = (x + xx * mu_ref[i:i + 1, :]).astype(o_ref.dtype)


def token_mix(x, x_shift, mu):
    m, d = x.shape
    tr = min(m, 256)
    row = pl.BlockSpec((tr, d), lambda i: (i, 0))
    n = mu.shape[0]
    return pl.pallas_call(
        _mix_kernel,
        grid=(m // tr,),
        in_specs=[row, row, pl.BlockSpec((n, d), lambda i: (0, 0))],
        out_specs=tuple(row for _ in range(n)),
        out_shape=tuple(jax.ShapeDtypeStruct((m, d), bf16) for _ in range(n)),
        compiler_params=_params("parallel"),
    )(x, x_shift, mu)


def _head_sum(x, lo):
    s_lo = jnp.sum(jnp.where(lo, x, 0.0), axis=-1, keepdims=True)
    s_hi = jnp.sum(jnp.where(lo, 0.0, x), axis=-1, keepdims=True)
    return jnp.where(lo, s_lo, s_hi)


def _softplus(z):
    return jnp.maximum(z, 0.0) + jnp.log(1.0 + jnp.exp(-jnp.abs(z)))


def _rwkv_prep(r, k, v, wl, al, prm, live):
    w0, a0, k_k, k_a = prm["w0"], prm["a0"], prm["k_k"], prm["k_a"]
    lo = lax.broadcasted_iota(jnp.int32, r.shape, 1) < RW_HEAD
    w_log = -_softplus(-(w0 + wl)) - 0.5
    lw = -jnp.exp(w_log)
    a = jax.nn.sigmoid(a0 + al)
    kk = k * k_k
    kk = kk / jnp.maximum(jnp.sqrt(_head_sum(kk * kk, lo)), 1e-12)
    kf = k * (1.0 + (a - 1.0) * k_a)
    if live is not None:
        lw = jnp.where(live, lw, 0.0)
        kk = jnp.where(live, kk, 0.0)
        kf = jnp.where(live, kf, 0.0)
        v = jnp.where(live, v, 0.0)
    return lw, -kk, kk * a, kf, v, lo


def _rwkv_post(y, r, kf, v, g, prm, lo):
    mean = _head_sum(y, lo) * (1.0 / RW_HEAD)
    yc = y - mean
    var = _head_sum(yc * yc, lo) * (1.0 / RW_HEAD)
    y = yc * lax.rsqrt(var + GN_EPS) * prm["gn_g"] + prm["gn_b"]
    y = y + _head_sum(r * kf * prm["r_k"], lo) * v
    return y * g


_RW_PARAM_NAMES = ("w0", "a0", "k_k", "k_a", "r_k", "gn_g", "gn_b")


def _rwkv_chunk_kernel(r_ref, k_ref, v_ref, g_ref, wl_ref, al_ref, prm_ref, o_ref, s_out_ref, s_scr, *, seq, chunk):
    prm = {name: prm_ref[i:i + 1, :] for i, name in enumerate(_RW_PARAM_NAMES)}
    c = chunk
    ti = lax.broadcasted_iota(jnp.int32, (c, c), 0)
    tj = lax.broadcasted_iota(jnp.int32, (c, c), 1)
    strict = tj < ti
    incl = tj <= ti
    tril = jnp.where(incl, 1.0, 0.0)
    eye = jnp.where(ti == tj, 1.0, 0.0)
    bi = lax.broadcasted_iota(jnp.int32, (LANES, LANES), 0) < RW_HEAD
    bj = lax.broadcasted_iota(jnp.int32, (LANES, LANES), 1) < RW_HEAD
    same_head = bi == bj
    s_scr[...] = jnp.zeros_like(s_scr)
    levels = int(math.log2(c)) - 1

    def body(ci, carry):
        rows = pl.ds(pl.multiple_of(ci * c, c), c)
        r, k, v, g = r_ref[rows, :], k_ref[rows, :], v_ref[rows, :], g_ref[rows, :]
        lw, av, bv, kf, v, lo = _rwkv_prep(r, k, v, wl_ref[rows, :], al_ref[rows, :], prm, None)
        cum = jnp.dot(tril, lw, preferred_element_type=f32, precision=lax.Precision.HIGHEST)
        g_inc = jnp.exp(cum)
        g_inv = jnp.exp(-cum)
        at = av * jnp.exp(cum - lw)
        bt = bv * g_inv
        kt = kf * g_inv
        rt = r * g_inc
        s0 = s_scr[...]
        a_s = _dot(at, s0, _NT)
        r_s = _dot(rt, s0, _NT)
        u = jnp.zeros((c, LANES), f32)
        y = jnp.zeros((c, LANES), f32)
        for head_lo in (True, False):
            mh = lo if head_lo else jnp.logical_not(lo)
            at_h = jnp.where(mh, at, 0.0)
            rt_h = jnp.where(mh, rt, 0.0)
            l_ab = jnp.where(strict, _dot(at_h, bt, _NT), 0.0)
            l_ak = jnp.where(strict, _dot(at_h, kt, _NT), 0.0)
            m_rb = jnp.where(incl, _dot(rt_h, bt, _NT), 0.0)
            m_rk = jnp.where(incl, _dot(rt_h, kt, _NT), 0.0)
            rhs = a_s + _dot(l_ak, v)
            x = l_ab
            inv = eye + x
            for _ in range(levels):
                x = _dot(x, x)
                inv = inv + _dot(inv, x)
            u_h = _dot(inv, rhs)
            y_h = r_s + _dot(m_rb, u_h) + _dot(m_rk, v)
            u = jnp.where(mh, u_h, u)
            y = jnp.where(mh, y_h, y)
        ds = _dot(u, bt, _TN) + _dot(v, kt, _TN)
        s_scr[...] = (s0 + jnp.where(same_head, ds, 0.0)) * g_inc[c - 1:c, :]
        o_ref[rows, :] = _rwkv_post(y, r, kf, v, g, prm, lo)
        return carry

    lax.fori_loop(0, seq // c, body, 0)
    s_out_ref[...] = s_scr[...]


def _rwkv_step_kernel(r_ref, k_ref, v_ref, g_ref, wl_ref, al_ref, prm_ref, s_ref, o_ref, s_out_ref, *, s_len, pairs):
    bi = lax.broadcasted_iota(jnp.int32, (LANES, LANES), 0)
    bj = lax.broadcasted_iota(jnp.int32, (LANES, LANES), 1)
    same_head = (bi < RW_HEAD) == (bj < RW_HEAD)
    eye = bi == bj
    for p in range(pairs):
        cs = slice(p * LANES, (p + 1) * LANES)
        prm = {name: prm_ref[i:i + 1, cs] for i, name in enumerate(_RW_PARAM_NAMES)}
        r, k, v, g = r_ref[:, cs], k_ref[:, cs], v_ref[:, cs], g_ref[:, cs]
        lw, av, bv, kf, v, lo = _rwkv_prep(r, k, v, wl_ref[:, cs], al_ref[:, cs], prm, None)
        w = jnp.exp(lw)
        s = s_ref[p]
        ys = []
        for t in range(s_len):
            row = slice(t, t + 1)
            sa = jnp.sum(s * av[row], axis=-1, keepdims=True)
            v_col = jnp.sum(jnp.where(eye, v[row], 0.0), axis=-1, keepdims=True)
            s = s * w[row] + jnp.where(same_head, sa * bv[row] + v_col * kf[row], 0.0)
            y_col = jnp.sum(s * r[row], axis=-1, keepdims=True)
            ys.append(jnp.sum(jnp.where(eye, y_col, 0.0), axis=0, keepdims=True))
        y = jnp.concatenate(ys, axis=0)
        o_ref[:, cs] = _rwkv_post(y, r, kf, v, g, prm, lo)
        s_out_ref[p] = s


def _rw_param_rows(w0, a0, k_k, k_a, r_k, gn_g, gn_b):
    d = w0.shape[0]
    rows = [w0, a0, k_k, k_a, r_k.reshape(d), gn_g, gn_b]
    rows += [jnp.zeros((d,), f32)] * (8 - len(rows))
    return jnp.stack(rows)


def rwkv_prompt(r, k, v, g, wl, al, prm, *, batch, seq, chunk=64):
    d = r.shape[1]
    pairs = d // LANES
    blk = pl.BlockSpec((seq, LANES), lambda b, p: (b, p))
    out, s_fin = pl.pallas_call(
        functools.partial(_rwkv_chunk_kernel, seq=seq, chunk=chunk),
        grid=(batch, pairs),
        in_specs=[blk] * 6 + [pl.BlockSpec((8, LANES), lambda b, p: (0, p))],
        out_specs=(blk, pl.BlockSpec((None, None, LANES, LANES), lambda b, p: (b, p, 0, 0))),
        out_shape=(jax.ShapeDtypeStruct((batch * seq, d), f32),
                   jax.ShapeDtypeStruct((batch, pairs, LANES, LANES), f32)),
        scratch_shapes=[pltpu.VMEM((LANES, LANES), f32)],
        compiler_params=_params("parallel", "parallel"),
    )(r, k, v, g, wl, al, prm)
    return out, _unpair_state(s_fin)


def rwkv_sample(r, k, v, g, wl, al, prm, state, *, batch, s_len, pairs_per_step=4):
    d = r.shape[1]
    pairs = d // LANES
    pps = min(pairs_per_step, pairs)
    width = pps * LANES

    def rows3(x):
        return x.reshape(batch, s_len, d)

    blk = pl.BlockSpec((None, s_len, width), lambda b, p: (b, 0, p))
    st = pl.BlockSpec((None, pps, LANES, LANES), lambda b, p: (b, p, 0, 0))
    out, s_fin = pl.pallas_call(
        functools.partial(_rwkv_step_kernel, s_len=s_len, pairs=pps),
        grid=(batch, pairs // pps),
        in_specs=[blk] * 6 + [pl.BlockSpec((8, width), lambda b, p: (0, p)), st],
        out_specs=(blk, st),
        out_shape=(jax.ShapeDtypeStruct((batch, s_len, d), f32),
                   jax.ShapeDtypeStruct((batch, pairs, LANES, LANES), f32)),
        compiler_params=_params("parallel", "parallel"),
    )(rows3(r), rows3(k), rows3(v), rows3(g), rows3(wl), rows3(al), prm, _pair_state(state))
    return out.reshape(batch * s_len, d), _unpair_state(s_fin)


def _pair_state(state):
    b, h, n, _ = state.shape
    s5 = state.astype(f32).reshape(b, h // 2, 2, n, n)
    eye2 = jnp.eye(2, dtype=f32)
    return (s5[:, :, :, :, None, :] * eye2[None, None, :, None, :, None]).reshape(b, h // 2, 2 * n, 2 * n)


def _unpair_state(paired):
    b, p, n2, _ = paired.shape
    n = n2 // 2
    s6 = paired.reshape(b, p, 2, n, 2, n)
    return jnp.stack([s6[:, :, 0, :, 0, :], s6[:, :, 1, :, 1, :]], axis=2).reshape(b, 2 * p, n, n)


def rwkv_mixer(x, xb_unused, x_shift, state, rw, *, batch, seq):
    (mu, w_rkv, w0, w1, w2, a0, a1, a2, g1, g2, k_k, k_a, r_k, gn_g, gn_b, w_o) = rw
    xr, xw, xk, xv, xa, xg = token_mix(x, x_shift, mu)
    r = matmul(xr, w_rkv, lead=0)
    k = matmul(xk, w_rkv, lead=1)
    v = matmul(xv, w_rkv, lead=2)
    wl = matmul(matmul(xw, w1, act="tanh"), w2)
    al = matmul(matmul(xa, a1), a2)
    g = matmul(matmul(xg, g1, act="sigmoid"), g2)
    prm = _rw_param_rows(w0, a0, k_k, k_a, r_k, gn_g, gn_b)
    if state is None:
        z, s_fin = rwkv_prompt(r, k, v, g, wl, al, prm, batch=batch, seq=seq)
    else:
        z, s_fin = rwkv_sample(r, k, v, g, wl, al, prm, state, batch=batch, s_len=seq)
    return matmul(z, w_o), s_fin


def kernel(x_prompt, x_sample, cache_a_k, cache_a_v, cache_b_k, cache_b_v, state_shift, state_wkv, w_in_att, w_out_att, att_sinks, rel_bias, rw_mu, rw_w_rkv, rw_w0, rw_w1, rw_w2, rw_a0, rw_a1, rw_a2, rw_g1, rw_g2, rw_k_k, rw_k_a, rw_r_k, rw_gn_g, rw_gn_b, rw_w_o, peer_w_q, peer_sub_keys, peer_u, peer_v, ln_g, ln_b):
    bp, seq, d = x_prompt.shape
    bs, s_len, _ = x_sample.shape
    xp = x_prompt.reshape(bp * seq, d)
    xs = x_sample.reshape(bs * s_len, d)

    def peer_and_ln(x, xb, layer, tables):
        y_t = peer_ffn_t(xb, peer_w_q, peer_sub_keys, tables[0], tables[1], layer)
        return add_ln(x, y_t, ln_g[layer, 1], ln_b[layer, 1], y_transposed=True)

    def peer_tables(layer):
        return peer_u[layer].astype(bf16), peer_v[layer].T.astype(bf16)

    sinks = jnp.concatenate([att_sinks.astype(f32), jnp.full((H_B,), NEG_INF, f32)])
    qkv_p = matmul(xp.astype(bf16), w_in_att)
    qkv_s = matmul(xs.astype(bf16), w_in_att)
    o_p = attn_prompt(qkv_p, _prompt_bias(rel_bias), sinks, batch=bp, seq=seq)
    o_s = attn_sample(qkv_s, cache_a_k, cache_a_v, cache_b_k, cache_b_v, _sample_bias(rel_bias, s_len), sinks,
                      batch=bs, s_len=s_len)
    mix_p = matmul(o_p, w_out_att)
    mix_s = matmul(o_s, w_out_att)

    def heads(z, rows, blk, n):
        return z.reshape(rows + (z.shape[-1],))[..., blk * HEAD_DIM:(blk + n) * HEAD_DIM].reshape(rows + (n, HEAD_DIM))

    la_p = min(BAND, seq)
    qp3 = qkv_p.reshape(bp, seq, -1)
    a_k_p = heads(qp3[:, -la_p:], (bp, la_p), KA_BLK, KV_A)
    a_v_p = heads(qp3[:, -la_p:], (bp, la_p), VA_BLK, KV_A)
    b_k_p = heads(qp3, (bp, seq), KB_BLK, H_B)
    b_v_p = heads(qp3, (bp, seq), VB_BLK, H_B)
    qs3 = qkv_s.reshape(bs, s_len, -1)
    a_k_s = jnp.concatenate([cache_a_k[:, s_len:], heads(qs3, (bs, s_len), KA_BLK, KV_A)], axis=1)
    a_v_s = jnp.concatenate([cache_a_v[:, s_len:], heads(qs3, (bs, s_len), VA_BLK, KV_A)], axis=1)
    b_k_s = jnp.concatenate([cache_b_k[:, s_len:], heads(qs3, (bs, s_len), KB_BLK, H_B)], axis=1)
    b_v_s = jnp.concatenate([cache_b_v[:, s_len:], heads(qs3, (bs, s_len), VB_BLK, H_B)], axis=1)

    xp, xpb = add_ln(xp, mix_p, ln_g[0, 0], ln_b[0, 0])
    xs, xsb = add_ln(xs, mix_s, ln_g[0, 0], ln_b[0, 0])
    tables = peer_tables(0)
    xp, xpb = peer_and_ln(xp, xpb, 0, tables)
    xs, xsb = peer_and_ln(xs, xsb, 0, tables)

    rw = (rw_mu, rw_w_rkv, rw_w0, rw_w1, rw_w2, rw_a0, rw_a1, rw_a2, rw_g1, rw_g2,
          rw_k_k, rw_k_a, rw_r_k, rw_gn_g, rw_gn_b, rw_w_o)
    xp3 = xp.reshape(bp, seq, d)
    xs3 = xs.reshape(bs, s_len, d)
    shift_in_p = jnp.concatenate([jnp.zeros((bp, 1, d), f32), xp3[:, :-1]], axis=1).reshape(bp * seq, d)
    shift_in_s = jnp.concatenate([state_shift[:, None, :].astype(f32), xs3[:, :-1]], axis=1).reshape(bs * s_len, d)
    shift_p, shift_s = xp3[:, -1], xs3[:, -1]
    mix_p, wkv_p = rwkv_mixer(xp, xpb, shift_in_p, None, rw, batch=bp, seq=seq)
    mix_s, wkv_s = rwkv_mixer(xs, xsb, shift_in_s, state_wkv, rw, batch=bs, seq=s_len)

    xp, xpb = add_ln(xp, mix_p, ln_g[1, 0], ln_b[1, 0])
    xs, xsb = add_ln(xs, mix_s, ln_g[1, 0], ln_b[1, 0])
    tables = peer_tables(1)
    xp, _ = peer_and_ln(xp, xpb, 1, tables)
    xs, _ = peer_and_ln(xs, xsb, 1, tables)

    return (xp.reshape(bp, seq, d), xs.reshape(bs, s_len, d), a_k_p, a_v_p, b_k_p, b_v_p, shift_p, wkv_p,
            a_k_s, a_v_s, b_k_s, b_v_s, shift_s, wkv_s)
```
